```python
import math
import jax, jax.numpy as jnp
from jax import lax
import numpy as np

D_MODEL = 2048
BATCH = 8
SEQ = 4096
DEPTH = 1

CHUNK = 64
D_MIX = D_MODEL
D_LRU = D_MIX // 2
N_LRU_HEADS = 8
LRU_HEAD_DIM = D_LRU // N_LRU_HEADS
D_SC = D_MIX - D_LRU
N_SC_GROUPS = 16
SC_GROUP_DIM = D_SC // N_SC_GROUPS
LRU_CONV_W = 4
SC_CONV_W = 3
RG_C = 8.0
D_FF = 5632
EPS = 1e-6
D_IN = 2 * D_LRU + 3 * D_SC

kernel_name = "hybrid_rglru_shortconv_macaron_block"


def rmsnorm(x, g):
    xf = x.astype(jnp.float32)
    y = xf * lax.rsqrt(jnp.mean(xf * xf, axis=-1, keepdims=True) + EPS)
    return (y * g.astype(jnp.float32)).astype(x.dtype)


def swiglu(x, w_gate, w_up, w_down):
    return (jax.nn.silu(x @ w_gate) * (x @ w_up)) @ w_down


def causal_dwconv(x, w, b=None):
    k_w = w.shape[0]
    s = x.shape[1]
    xp = jnp.pad(x, ((0, 0), (k_w - 1, 0), (0, 0)))
    y = xp[:, 0:s] * w[0]
    for k in range(1, k_w):
        y = y + xp[:, k:k + s] * w[k]
    if b is not None:
        y = y + b
    return y


def rg_lru(x, w_r, b_r, w_i, b_i, lam):
    bsz, s, _ = x.shape
    xh = x.reshape(bsz, s, N_LRU_HEADS, LRU_HEAD_DIM)
    r = jax.nn.sigmoid(jnp.einsum('bshd,hde->bshe', xh, w_r) + b_r).reshape(bsz, s, D_LRU)
    i = jax.nn.sigmoid(jnp.einsum('bshd,hde->bshe', xh, w_i) + b_i).reshape(bsz, s, D_LRU)
    log_a = -RG_C * r.astype(jnp.float32) * jax.nn.softplus(-lam.astype(jnp.float32))
    a = jnp.exp(log_a)
    mult = jnp.sqrt(-jnp.expm1(2.0 * log_a))
    b = mult * (i.astype(jnp.float32) * x.astype(jnp.float32))

    def combine(left, right):
        a_l, b_l = left
        a_r, b_r_ = right
        return a_l * a_r, a_r * b_l + b_r_

    _, h = lax.associative_scan(combine, (a, b), axis=1)
    return h.astype(x.dtype)


def setup_inputs(seed: int = 0) -> dict:
    key = jax.random.key(seed)
    ks = jax.random.split(key, 32)
    f32 = jnp.float32

    def nrm(k, shape, fan_in):
        return jax.random.normal(k, shape, f32) * (fan_in ** -0.5)

    def gain(k, shape):
        return 1.0 + 0.05 * jax.random.normal(k, shape, f32)

    L = DEPTH
    u = jax.random.uniform(ks[20], (L, D_LRU), f32, 0.9, 0.999)
    a0 = u ** (1.0 / RG_C)
    lam = jnp.log(a0) - jnp.log1p(-a0)
    return {
        "x": jax.random.normal(ks[0], (BATCH, SEQ, D_MODEL), f32),
        "ffn1_pre_g": gain(ks[1], (L, D_MODEL)),
        "ffn1_w_gate": nrm(ks[2], (L, D_MODEL, D_FF), D_MODEL),
        "ffn1_w_up": nrm(ks[3], (L, D_MODEL, D_FF), D_MODEL),
        "ffn1_w_down": nrm(ks[4], (L, D_FF, D_MODEL), D_FF),
        "ffn1_post_g": gain(ks[5], (L, D_MODEL)),
        "mix_pre_g": gain(ks[6], (L, D_MODEL)),
        "w_in": nrm(ks[7], (L, D_MODEL, D_IN), D_MODEL),
        "lru_conv_w": nrm(ks[8], (L, LRU_CONV_W, D_LRU), LRU_CONV_W),
        "lru_conv_b": 0.01 * jax.random.normal(ks[9], (L, D_LRU), f32),
        "lru_w_r": nrm(ks[10], (L, N_LRU_HEADS, LRU_HEAD_DIM, LRU_HEAD_DIM), LRU_HEAD_DIM),
        "lru_b_r": 0.01 * jax.random.normal(ks[11], (L, N_LRU_HEADS, LRU_HEAD_DIM), f32),
        "lru_w_i": nrm(ks[12], (L, N_LRU_HEADS, LRU_HEAD_DIM, LRU_HEAD_DIM), LRU_HEAD_DIM),
        "lru_b_i": 0.01 * jax.random.normal(ks[13], (L, N_LRU_HEADS, LRU_HEAD_DIM), f32),
        "lru_lambda": lam,
        "sc_conv_w": nrm(ks[14], (L, SC_CONV_W, D_SC), SC_CONV_W),
        "w_out": nrm(ks[15], (L, D_MIX, D_MODEL), D_MIX),
        "mix_post_g": gain(ks[16], (L, D_MODEL)),
        "ffn2_pre_g": gain(ks[17], (L, D_MODEL)),
        "ffn2_w_gate": nrm(ks[18], (L, D_MODEL, D_FF), D_MODEL),
        "ffn2_w_up": nrm(ks[19], (L, D_MODEL, D_FF), D_MODEL),
        "ffn2_w_down": nrm(ks[21], (L, D_FF, D_MODEL), D_FF),
        "ffn2_post_g": gain(ks[22], (L, D_MODEL)),
    }


def reference(x, ffn1_pre_g, ffn1_w_gate, ffn1_w_up, ffn1_w_down, ffn1_post_g,
              mix_pre_g, w_in, lru_conv_w, lru_conv_b, lru_w_r, lru_b_r, lru_w_i, lru_b_i,
              lru_lambda, sc_conv_w, w_out, mix_post_g,
              ffn2_pre_g, ffn2_w_gate, ffn2_w_up, ffn2_w_down, ffn2_post_g):
    h = x
    for l in range(DEPTH):
        f = swiglu(rmsnorm(h, ffn1_pre_g[l]), ffn1_w_gate[l], ffn1_w_up[l], ffn1_w_down[l])
        h = h + 0.5 * rmsnorm(f, ffn1_post_g[l])

        u = rmsnorm(h, mix_pre_g[l])
        z = u @ w_in[l]
        o = np.cumsum([0, D_LRU, D_LRU, D_SC, D_SC, D_SC]).tolist()
        lru_x = z[..., o[0]:o[1]]
        lru_gate = z[..., o[1]:o[2]]
        sc_b = z[..., o[2]:o[3]]
        sc_c = z[..., o[3]:o[4]]
        sc_x = z[..., o[4]:o[5]]

        xc = causal_dwconv(lru_x, lru_conv_w[l], lru_conv_b[l])
        y_lru = rg_lru(xc, lru_w_r[l], lru_b_r[l], lru_w_i[l], lru_b_i[l], lru_lambda[l])
        y_lru = y_lru * jax.nn.gelu(lru_gate)

        y_sc = sc_b * causal_dwconv(sc_c * sc_x, sc_conv_w[l])

        y = jnp.concatenate([y_lru, y_sc], axis=-1) @ w_out[l]
        h = h + rmsnorm(y, mix_post_g[l])

        f = swiglu(rmsnorm(h, ffn2_pre_g[l]), ffn2_w_gate[l], ffn2_w_up[l], ffn2_w_down[l])
        h = h + 0.5 * rmsnorm(f, ffn2_post_g[l])
    return h
```

```python
import functools
import math

import jax
import jax.numpy as jnp
from jax import lax
from jax.experimental import pallas as pl
from jax.experimental.pallas import tpu as pltpu

EPS = 1e-6
RG_C = 8.0
SUBLANES = 8
VMEM_LIMIT_BYTES = 56 * 1024 * 1024

FFN_ROW_TILE = 512
FFN_FF_TILE = 512
FFN_OUT_COL_TILE = 512
MIX_SEQ_TILE = 256


def _rms_scale(x):
    return lax.rsqrt(jnp.mean(x * x, axis=-1, keepdims=True) + EPS)


def _ffn_kernel(x_ref, pre_g_ref, wg_ref, wu_ref, wd_ref, post_g_ref, o_ref, xn_ref,
                *, out_col_tile):
    j = pl.program_id(1)
    n_j = pl.num_programs(1)
    d_model = o_ref.shape[-1]

    @pl.when(j == 0)
    def _():
        x = x_ref[...]
        xn_ref[...] = (x * _rms_scale(x) * pre_g_ref[...]).astype(xn_ref.dtype)

    xn = xn_ref[...]
    gate = jnp.dot(xn, wg_ref[...], preferred_element_type=jnp.float32)
    up = jnp.dot(xn, wu_ref[...], preferred_element_type=jnp.float32)
    hid = ((gate * jax.nn.sigmoid(gate)) * up).astype(jnp.bfloat16)

    for c in range(0, d_model, out_col_tile):
        cols = slice(c, c + out_col_tile)
        part = jnp.dot(hid, wd_ref[:, cols], preferred_element_type=jnp.float32)

        @pl.when(j == 0)
        def _():
            o_ref[:, cols] = part

        @pl.when(j > 0)
        def _():
            o_ref[:, cols] += part

    @pl.when(j == n_j - 1)
    def _():
        f = o_ref[...]
        o_ref[...] = x_ref[...] + 0.5 * (f * _rms_scale(f) * post_g_ref[...])


def _ffn(x2d, pre_g, w_gate, w_up, w_down, post_g):
    m, d = x2d.shape
    f = w_gate.shape[1]
    tm, tf = min(FFN_ROW_TILE, m), min(FFN_FF_TILE, f)
    assert m % tm == 0 and f % tf == 0
    out_col_tile = min(FFN_OUT_COL_TILE, d)
    assert d % out_col_tile == 0
    row = lambda i, j: (i, 0)
    const = lambda i, j: (0, 0)
    return pl.pallas_call(
        functools.partial(_ffn_kernel, out_col_tile=out_col_tile),
        name="ffn",
        grid=(m // tm, f // tf),
        in_specs=[
            pl.BlockSpec((tm, d), row),
            pl.BlockSpec((1, d), const),
            pl.BlockSpec((d, tf), lambda i, j: (0, j)),
            pl.BlockSpec((d, tf), lambda i, j: (0, j)),
            pl.BlockSpec((tf, d), lambda i, j: (j, 0)),
            pl.BlockSpec((1, d), const),
        ],
        out_specs=pl.BlockSpec((tm, d), row),
        out_shape=jax.ShapeDtypeStruct((m, d), jnp.float32),
        scratch_shapes=[pltpu.VMEM((tm, d), jnp.bfloat16)],
        compiler_params=pltpu.CompilerParams(
            dimension_semantics=("parallel", "arbitrary"),
            vmem_limit_bytes=VMEM_LIMIT_BYTES),
    )(x2d, pre_g.reshape(1, d), w_gate, w_up, w_down, post_g.reshape(1, d))


def _shift_rows(cur, tail, k):
    rolled = pltpu.roll(cur, k, axis=0)
    tail_rolled = pltpu.roll(tail, k, axis=0)
    row = lax.broadcasted_iota(jnp.int32, tail.shape, 0)
    head = jnp.where(row < k, tail_rolled, rolled[:SUBLANES])
    return jnp.concatenate([head, rolled[SUBLANES:]], axis=0)


def _causal_conv(cur, tail, w_ref):
    k_w = w_ref.shape[0]
    y = _shift_rows(cur, tail, k_w - 1) * w_ref[0:1, :]
    for k in range(1, k_w - 1):
        y = y + _shift_rows(cur, tail, k_w - 1 - k) * w_ref[k:k + 1, :]
    return y + cur * w_ref[k_w - 1:k_w, :]


def _linear_scan(a, b, h0):
    t = a.shape[0]
    row = lax.broadcasted_iota(jnp.int32, a.shape, 0)
    s = 1
    while s < t:
        keep = row >= s
        b = jnp.where(keep, a * pltpu.roll(b, s, axis=0) + b, b)
        a = jnp.where(keep, a * pltpu.roll(a, s, axis=0), a)
        s *= 2
    return a * h0 + b


def _gelu_tanh(x):
    c = math.sqrt(2.0 / math.pi)
    return x * (0.5 * (1.0 + jnp.tanh(c * (x + 0.044715 * (x * x * x)))))


def _mixer_kernel(h_ref, pre_g_ref, w_in_ref, lru_cw_ref, lru_cb_ref, w_ri_ref, b_r_ref, b_i_ref,
                  lam_ref, sc_cw_ref, w_out_ref, post_g_ref, o_ref,
                  lru_tail_ref, sc_tail_ref, h_carry_ref, y_ref):
    d_lru = lam_ref.shape[-1]
    d_sc = sc_cw_ref.shape[-1]
    n_heads, head_dim, _ = w_ri_ref.shape
    t = h_ref.shape[0]

    @pl.when(pl.program_id(1) == 0)
    def _():
        lru_tail_ref[...] = jnp.zeros_like(lru_tail_ref)
        sc_tail_ref[...] = jnp.zeros_like(sc_tail_ref)
        h_carry_ref[...] = jnp.zeros_like(h_carry_ref)

    x = h_ref[...]
    u = (x * _rms_scale(x) * pre_g_ref[...]).astype(jnp.bfloat16)

    def proj(col0, width):
        return jnp.dot(u, w_in_ref[:, col0:col0 + width], preferred_element_type=jnp.float32)

    lru_x = proj(0, d_lru)
    xc = _causal_conv(lru_x, lru_tail_ref[...], lru_cw_ref) + lru_cb_ref[...]
    lru_tail_ref[...] = lru_x[t - SUBLANES:, :]

    neg_lam = -lam_ref[...]
    softplus_neg_lam = jnp.maximum(neg_lam, 0.0) + jnp.log1p(jnp.exp(-jnp.abs(neg_lam)))
    r_parts, i_parts = [], []
    for hd in range(n_heads):
        cols = slice(hd * head_dim, (hd + 1) * head_dim)
        ri = jnp.dot(xc[:, cols].astype(jnp.bfloat16), w_ri_ref[hd],
                     preferred_element_type=jnp.float32)
        r_parts.append(jax.nn.sigmoid(ri[:, :head_dim] + b_r_ref[:, cols]))
        i_parts.append(jax.nn.sigmoid(ri[:, head_dim:] + b_i_ref[:, cols]))
    r = jnp.concatenate(r_parts, axis=-1)
    i = jnp.concatenate(i_parts, axis=-1)
    log_a = (-RG_C) * r * softplus_neg_lam
    a = jnp.exp(log_a)
    th = jnp.tanh(log_a)
    b = jnp.sqrt((-2.0 * th) / (1.0 - th)) * (i * xc)
    h_seq = _linear_scan(a, b, h_carry_ref[SUBLANES - 1:SUBLANES, :])
    h_carry_ref[...] = h_seq[t - SUBLANES:, :]
    lru_gate = proj(d_lru, d_lru)
    y_ref[:, :d_lru] = (h_seq * _gelu_tanh(lru_gate)).astype(y_ref.dtype)

    sc_b = proj(2 * d_lru, d_sc)
    sc_c = proj(2 * d_lru + d_sc, d_sc)
    sc_x = proj(2 * d_lru + 2 * d_sc, d_sc)
    v = sc_c * sc_x
    y_sc = sc_b * _causal_conv(v, sc_tail_ref[...], sc_cw_ref)
    sc_tail_ref[...] = v[t - SUBLANES:, :]
    y_ref[:, d_lru:] = y_sc.astype(y_ref.dtype)

    y = jnp.dot(y_ref[...], w_out_ref[...], preferred_element_type=jnp.float32)
    o_ref[...] = x + y * _rms_scale(y) * post_g_ref[...]


def _mixer(h, pre_g, w_in, lru_conv_w, lru_conv_b, w_ri, b_r, b_i, lam, sc_conv_w, w_out, post_g):
    bsz, seq, d = h.shape
    d_in = w_in.shape[1]
    d_lru = lam.shape[-1]
    d_sc = sc_conv_w.shape[-1]
    d_mix = d_lru + d_sc
    t = min(MIX_SEQ_TILE, seq)
    assert seq % t == 0 and t % SUBLANES == 0
    tile = lambda b, s: (b, s, 0)
    const2 = lambda b, s: (0, 0)
    const3 = lambda b, s: (0, 0, 0)
    resident = dict(pipeline_mode=pl.Buffered(1))
    return pl.pallas_call(
        _mixer_kernel,
        name="mixer",
        grid=(bsz, seq // t),
        in_specs=[
            pl.BlockSpec((None, t, d), tile),
            pl.BlockSpec((1, d), const2),
            pl.BlockSpec((d, d_in), const2, **resident),
            pl.BlockSpec(lru_conv_w.shape, const2),
            pl.BlockSpec((1, d_lru), const2),
            pl.BlockSpec(w_ri.shape, const3, **resident),
            pl.BlockSpec((1, d_lru), const2),
            pl.BlockSpec((1, d_lru), const2),
            pl.BlockSpec((1, d_lru), const2),
            pl.BlockSpec(sc_conv_w.shape, const2),
            pl.BlockSpec((d_mix, d), const2, **resident),
            pl.BlockSpec((1, d), const2),
        ],
        out_specs=pl.BlockSpec((None, t, d), tile),
        out_shape=jax.ShapeDtypeStruct((bsz, seq, d), jnp.float32),
        scratch_shapes=[
            pltpu.VMEM((SUBLANES, d_lru), jnp.float32),
            pltpu.VMEM((SUBLANES, d_sc), jnp.float32),
            pltpu.VMEM((SUBLANES, d_lru), jnp.float32),
            pltpu.VMEM((t, d_mix), jnp.bfloat16),
        ],
        compiler_params=pltpu.CompilerParams(
            dimension_semantics=("parallel", "arbitrary"),
            vmem_limit_bytes=VMEM_LIMIT_BYTES),
    )(h, pre_g.reshape(1, d), w_in, lru_conv_w, lru_conv_b.reshape(1, d_lru), w_ri,
      b_r.reshape(1, d_lru), b_i.reshape(1, d_lru), lam.reshape(1, d_lru), sc_conv_w, w_out,
      post_g.reshape(1, d))


def kernel(x, ffn1_pre_g, ffn1_w_gate, ffn1_w_up, ffn1_w_down, ffn1_post_g, mix_pre_g, w_in, lru_conv_w, lru_conv_b, lru_w_r, lru_b_r, lru_w_i, lru_b_i, lru_lambda, sc_conv_w, w_out, mix_post_g, ffn2_pre_g, ffn2_w_gate, ffn2_w_up, ffn2_w_down, ffn2_post_g):
    bsz, seq, d = x.shape
    bf16 = lambda w: w.astype(jnp.bfloat16)
    h = x
    for l in range(ffn1_pre_g.shape[0]):
        h2d = _ffn(h.reshape(bsz * seq, d), ffn1_pre_g[l], bf16(ffn1_w_gate[l]), bf16(ffn1_w_up[l]),
                   bf16(ffn1_w_down[l]), ffn1_post_g[l])
        w_ri = bf16(jnp.concatenate([lru_w_r[l], lru_w_i[l]], axis=-1))
        h = _mixer(h2d.reshape(bsz, seq, d), mix_pre_g[l], bf16(w_in[l]), lru_conv_w[l],
                   lru_conv_b[l], w_ri, lru_b_r[l], lru_b_i[l], lru_lambda[l], sc_conv_w[l],
                   bf16(w_out[l]), mix_post_g[l])
        h2d = _ffn(h.reshape(bsz * seq, d), ffn2_pre_g[l], bf16(ffn2_w_gate[l]), bf16(ffn2_w_up[l]),
                   bf16(ffn2_w_down[l]), ffn2_post_g[l])
        h = h2d.reshape(bsz, seq, d)
    return h
```

```python
import functools
import math

import jax
import jax.numpy as jnp
from jax import lax
from jax.experimental import pallas as pl
from jax.experimental.pallas import tpu as pltpu

EPS = 1e-6
RG_C = 8.0
SUBLANES = 8
VMEM_LIMIT_BYTES = 56 * 1024 * 1024

FFN_ROW_TILE = 512
FFN_FF_TILE = 512
FFN_OUT_COL_TILE = 512
MIX_SEQ_TILE = 256


def _rms_scale(x):
    return lax.rsqrt(jnp.mean(x * x, axis=-1, keepdims=True) + EPS)


def _ffn_kernel(x_ref, pre_g_ref, wg_ref, wu_ref, wd_ref, post_g_ref, o_ref, xn_ref, hid_ref,
                *, out_col_tile):
    j = pl.program_id(1)
    n_ff = pl.num_programs(1) - 1
    d_model = o_ref.shape[-1]

    def hidden_chunk(slot):
        xn = xn_ref[...]
        gate = jnp.dot(xn, wg_ref[...], preferred_element_type=jnp.float32)
        up = jnp.dot(xn, wu_ref[...], preferred_element_type=jnp.float32)
        hid_ref[slot] = ((gate * jax.nn.sigmoid(gate)) * up).astype(hid_ref.dtype)

    def down_project(slot):
        hid = hid_ref[slot]
        for c in range(0, d_model, out_col_tile):
            cols = slice(c, c + out_col_tile)
            o_ref[:, cols] += jnp.dot(hid, wd_ref[:, cols], preferred_element_type=jnp.float32)

    slot = lax.rem(j, 2)

    @pl.when(j == 0)
    def _():
        x = x_ref[...]
        xn_ref[...] = (x * _rms_scale(x) * pre_g_ref[...]).astype(xn_ref.dtype)
        o_ref[...] = jnp.zeros_like(o_ref)
        hidden_chunk(slot)

    @pl.when(jnp.logical_and(j > 0, j < n_ff))
    def _():
        down_project(1 - slot)
        hidden_chunk(slot)

    @pl.when(j == n_ff)
    def _():
        down_project(1 - slot)
        f = o_ref[...]
        o_ref[...] = x_ref[...] + 0.5 * (f * _rms_scale(f) * post_g_ref[...])


def _ffn(x2d, pre_g, w_gate, w_up, w_down, post_g):
    m, d = x2d.shape
    f = w_gate.shape[1]
    tm, tf = min(FFN_ROW_TILE, m), min(FFN_FF_TILE, f)
    assert m % tm == 0 and f % tf == 0
    n_ff = f // tf
    out_col_tile = min(FFN_OUT_COL_TILE, d)
    assert d % out_col_tile == 0
    row = lambda i, j: (i, 0)
    const = lambda i, j: (0, 0)
    up_chunk = lambda i, j: (0, jnp.minimum(j, n_ff - 1))
    down_chunk = lambda i, j: (jnp.maximum(j - 1, 0), 0)
    return pl.pallas_call(
        functools.partial(_ffn_kernel, out_col_tile=out_col_tile),
        name="ffn",
        grid=(m // tm, n_ff + 1),
        in_specs=[
            pl.BlockSpec((tm, d), row),
            pl.BlockSpec((1, d), const),
            pl.BlockSpec((d, tf), up_chunk),
            pl.BlockSpec((d, tf), up_chunk),
            pl.BlockSpec((tf, d), down_chunk),
            pl.BlockSpec((1, d), const),
        ],
        out_specs=pl.BlockSpec((tm, d), row),
        out_shape=jax.ShapeDtypeStruct((m, d), jnp.float32),
        scratch_shapes=[
            pltpu.VMEM((tm, d), jnp.bfloat16),
            pltpu.VMEM((2, tm, tf), jnp.bfloat16),
        ],
        compiler_params=pltpu.CompilerParams(
            dimension_semantics=("parallel", "arbitrary"),
            vmem_limit_bytes=VMEM_LIMIT_BYTES),
    )(x2d, pre_g.reshape(1, d), w_gate, w_up, w_down, post_g.reshape(1, d))


def _shift_rows(cur, tail, k):
    rolled = pltpu.roll(cur, k, axis=0)
    tail_rolled = pltpu.roll(tail, k, axis=0)
    row = lax.broadcasted_iota(jnp.int32, tail.shape, 0)
    head = jnp.where(row < k, tail_rolled, rolled[:SUBLANES])
    return jnp.concatenate([head, rolled[SUBLANES:]], axis=0)


def _causal_conv(cur, tail, w_ref):
    k_w = w_ref.shape[0]
    y = _shift_rows(cur, tail, k_w - 1) * w_ref[0:1, :]
    for k in range(1, k_w - 1):
        y = y + _shift_rows(cur, tail, k_w - 1 - k) * w_ref[k:k + 1, :]
    return y + cur * w_ref[k_w - 1:k_w, :]


def _linear_scan(a, b, h0):
    t = a.shape[0]
    row = lax.broadcasted_iota(jnp.int32, a.shape, 0)
    s = 1
    while s < t:
        keep = row >= s
        b = jnp.where(keep, a * pltpu.roll(b, s, axis=0) + b, b)
        a = jnp.where(keep, a * pltpu.roll(a, s, axis=0), a)
        s *= 2
    return a * h0 + b


def _gelu_tanh(x):
    c = math.sqrt(2.0 / math.pi)
    return x * (0.5 * (1.0 + jnp.tanh(c * (x + 0.044715 * (x * x * x)))))


def _mixer_kernel(h_ref, pre_g_ref, w_in_ref, lru_cw_ref, lru_cb_ref, w_ri_ref, b_r_ref, b_i_ref,
                  lam_ref, sc_cw_ref, w_out_ref, post_g_ref, o_ref,
                  lru_tail_ref, sc_tail_ref, h_carry_ref, y_ref):
    d_lru = lam_ref.shape[-1]
    d_sc = sc_cw_ref.shape[-1]
    n_heads, head_dim, _ = w_ri_ref.shape
    t = h_ref.shape[0]

    @pl.when(pl.program_id(1) == 0)
    def _():
        lru_tail_ref[...] = jnp.zeros_like(lru_tail_ref)
        sc_tail_ref[...] = jnp.zeros_like(sc_tail_ref)
        h_carry_ref[...] = jnp.zeros_like(h_carry_ref)

    x = h_ref[...]
    u = (x * _rms_scale(x) * pre_g_ref[...]).astype(jnp.bfloat16)

    def proj(col0, width):
        return jnp.dot(u, w_in_ref[:, col0:col0 + width], preferred_element_type=jnp.float32)

    lru_x = proj(0, d_lru)
    xc = _causal_conv(lru_x, lru_tail_ref[...], lru_cw_ref) + lru_cb_ref[...]
    lru_tail_ref[...] = lru_x[t - SUBLANES:, :]

    neg_lam = -lam_ref[...]
    softplus_neg_lam = jnp.maximum(neg_lam, 0.0) + jnp.log1p(jnp.exp(-jnp.abs(neg_lam)))
    r_parts, i_parts = [], []
    for hd in range(n_heads):
        cols = slice(hd * head_dim, (hd + 1) * head_dim)
        ri = jnp.dot(xc[:, cols].astype(jnp.bfloat16), w_ri_ref[hd],
                     preferred_element_type=jnp.float32)
        r_parts.append(jax.nn.sigmoid(ri[:, :head_dim] + b_r_ref[:, cols]))
        i_parts.append(jax.nn.sigmoid(ri[:, head_dim:] + b_i_ref[:, cols]))
    r = jnp.concatenate(r_parts, axis=-1)
    i = jnp.concatenate(i_parts, axis=-1)
    log_a = (-RG_C) * r * softplus_neg_lam
    a = jnp.exp(log_a)
    th = jnp.tanh(log_a)
    b = jnp.sqrt((-2.0 * th) / (1.0 - th)) * (i * xc)
    h_seq = _linear_scan(a, b, h_carry_ref[SUBLANES - 1:SUBLANES, :])
    h_carry_ref[...] = h_seq[t - SUBLANES:, :]
    lru_gate = proj(d_lru, d_lru)
    y_ref[:, :d_lru] = (h_seq * _gelu_tanh(lru_gate)).astype(y_ref.dtype)

    sc_b = proj(2 * d_lru, d_sc)
    sc_c = proj(2 * d_lru + d_sc, d_sc)
    sc_x = proj(2 * d_lru + 2 * d_sc, d_sc)
    v = sc_c * sc_x
    y_sc = sc_b * _causal_conv(v, sc_tail_ref[...], sc_cw_ref)
    sc_tail_ref[...] = v[t - SUBLANES:, :]
    y_ref[:, d_lru:] = y_sc.astype(y_ref.dtype)

    y = jnp.dot(y_ref[...], w_out_ref[...], preferred_element_type=jnp.float32)
    o_ref[...] = x + y * _rms_scale(y) * post_g_ref[...]


def _mixer(h, pre_g, w_in, lru_conv_w, lru_conv_b, w_ri, b_r, b_i, lam, sc_conv_w, w_out, post_g):
    bsz, seq, d = h.shape
    d_in = w_in.shape[1]
    d_lru = lam.shape[-1]
    d_sc = sc_conv_w.shape[-1]
    d_mix = d_lru + d_sc
    t = min(MIX_SEQ_TILE, seq)
    assert seq % t == 0 and t % SUBLANES == 0
    tile = lambda b, s: (b, s, 0)
    const2 = lambda b, s: (0, 0)
    const3 = lambda b, s: (0, 0, 0)
    resident = dict(pipeline_mode=pl.Buffered(1))
    return pl.pallas_call(
        _mixer_kernel,
        name="mixer",
        grid=(bsz, seq // t),
        in_specs=[
            pl.BlockSpec((None, t, d), tile),
            pl.BlockSpec((1, d), const2),
            pl.BlockSpec((d, d_in), const2, **resident),
            pl.BlockSpec(lru_conv_w.shape, const2),
            pl.BlockSpec((1, d_lru), const2),
            pl.BlockSpec(w_ri.shape, const3, **resident),
            pl.BlockSpec((1, d_lru), const2),
            pl.BlockSpec((1, d_lru), const2),
            pl.BlockSpec((1, d_lru), const2),
            pl.BlockSpec(sc_conv_w.shape, const2),
            pl.BlockSpec((d_mix, d), const2, **resident),
            pl.BlockSpec((1, d), const2),
        ],
        out_specs=pl.BlockSpec((None, t, d), tile),
        out_shape=jax.ShapeDtypeStruct((bsz, seq, d), jnp.float32),
        scratch_shapes=[
            pltpu.VMEM((SUBLANES, d_lru), jnp.float32),
            pltpu.VMEM((SUBLANES, d_sc), jnp.float32),
            pltpu.VMEM((SUBLANES, d_lru), jnp.float32),
            pltpu.VMEM((t, d_mix), jnp.bfloat16),
        ],
        compiler_params=pltpu.CompilerParams(
            dimension_semantics=("parallel", "arbitrary"),
            vmem_limit_bytes=VMEM_LIMIT_BYTES),
    )(h, pre_g.reshape(1, d), w_in, lru_conv_w, lru_conv_b.reshape(1, d_lru), w_ri,
      b_r.reshape(1, d_lru), b_i.reshape(1, d_lru), lam.reshape(1, d_lru), sc_conv_w, w_out,
      post_g.reshape(1, d))


def kernel(x, ffn1_pre_g, ffn1_w_gate, ffn1_w_up, ffn1_w_down, ffn1_post_g, mix_pre_g, w_in, lru_conv_w, lru_conv_b, lru_w_r, lru_b_r, lru_w_i, lru_b_i, lru_lambda, sc_conv_w, w_out, mix_post_g, ffn2_pre_g, ffn2_w_gate, ffn2_w_up, ffn2_w_down, ffn2_post_g):
    bsz, seq, d = x.shape
    bf16 = lambda w: w.astype(jnp.bfloat16)
    h = x
    for l in range(ffn1_pre_g.shape[0]):
        h2d = _ffn(h.reshape(bsz * seq, d), ffn1_pre_g[l], bf16(ffn1_w_gate[l]), bf16(ffn1_w_up[l]),
                   bf16(ffn1_w_down[l]), ffn1_post_g[l])
        w_ri = bf16(jnp.concatenate([lru_w_r[l], lru_w_i[l]], axis=-1))
        h = _mixer(h2d.reshape(bsz, seq, d), mix_pre_g[l], bf16(w_in[l]), lru_conv_w[l],
                   lru_conv_b[l], w_ri, lru_b_r[l], lru_b_i[l], lru_lambda[l], sc_conv_w[l],
                   bf16(w_out[l]), mix_post_g[l])
        h2d = _ffn(h.reshape(bsz * seq, d), ffn2_pre_g[l], bf16(ffn2_w_gate[l]), bf16(ffn2_w_up[l]),
                   bf16(ffn2_w_down[l]), ffn2_post_g[l])
        h = h2d.reshape(bsz, seq, d)
    return h
```

```python
import functools
import math

import jax
import jax.numpy as jnp
from jax import lax
from jax.experimental import pallas as pl
from jax.experimental.pallas import tpu as pltpu

EPS = 1e-6
RG_C = 8.0
SUBLANES = 8
VMEM_LIMIT_BYTES = 60 * 1024 * 1024

FFN_ROW_TILE = 1024
FFN_ROW_SUBTILE = 512
FFN_FF_TILE = 512
FFN_OUT_COL_TILE = 512
MIX_SEQ_TILE = 256


def _rms_scale(x):
    return lax.rsqrt(jnp.mean(x * x, axis=-1, keepdims=True) + EPS)


def _ffn_kernel(x_ref, pre_g_ref, wg_ref, wu_ref, wd_ref, post_g_ref, o_ref, xn_ref, hid_ref,
                *, row_sub, out_col_tile):
    j = pl.program_id(1)
    n_ff = pl.num_programs(1) - 1
    d_model = o_ref.shape[-1]

    row_subs = [slice(r, r + row_sub) for r in range(0, o_ref.shape[0], row_sub)]

    def hidden_chunk(slot):
        for rows in row_subs:
            xn = xn_ref[rows, :]
            gate = jnp.dot(xn, wg_ref[...], preferred_element_type=jnp.float32)
            up = jnp.dot(xn, wu_ref[...], preferred_element_type=jnp.float32)
            hid_ref[slot, rows, :] = ((gate * jax.nn.sigmoid(gate)) * up).astype(hid_ref.dtype)

    def down_project(slot):
        for rows in row_subs:
            hid = hid_ref[slot, rows, :]
            for c in range(0, d_model, out_col_tile):
                cols = slice(c, c + out_col_tile)
                o_ref[rows, cols] += jnp.dot(hid, wd_ref[:, cols],
                                             preferred_element_type=jnp.float32)

    slot = lax.rem(j, 2)

    @pl.when(j == 0)
    def _():
        x = x_ref[...]
        xn_ref[...] = (x * _rms_scale(x) * pre_g_ref[...]).astype(xn_ref.dtype)
        o_ref[...] = jnp.zeros_like(o_ref)
        hidden_chunk(slot)

    @pl.when(jnp.logical_and(j > 0, j < n_ff))
    def _():
        down_project(1 - slot)
        hidden_chunk(slot)

    @pl.when(j == n_ff)
    def _():
        down_project(1 - slot)
        f = o_ref[...]
        o_ref[...] = x_ref[...] + 0.5 * (f * _rms_scale(f) * post_g_ref[...])


def _ffn_up_weight(w, tf):
    d, f = w.shape
    return w.reshape(d, f // tf, tf).transpose(1, 0, 2).astype(jnp.bfloat16)


def _ffn(x2d, pre_g, w_gate, w_up, w_down, post_g):
    m, d = x2d.shape
    f = w_gate.shape[1]
    tm, tf = min(FFN_ROW_TILE, m), min(FFN_FF_TILE, f)
    assert m % tm == 0 and f % tf == 0
    n_ff = f // tf
    row_sub = min(FFN_ROW_SUBTILE, tm)
    out_col_tile = min(FFN_OUT_COL_TILE, d)
    assert tm % row_sub == 0 and d % out_col_tile == 0
    row = lambda i, j: (i, 0)
    const = lambda i, j: (0, 0)
    up_chunk = lambda i, j: (jnp.minimum(j, n_ff - 1), 0, 0)
    down_chunk = lambda i, j: (jnp.maximum(j - 1, 0), 0)
    return pl.pallas_call(
        functools.partial(_ffn_kernel, row_sub=row_sub, out_col_tile=out_col_tile),
        name="ffn",
        grid=(m // tm, n_ff + 1),
        in_specs=[
            pl.BlockSpec((tm, d), row),
            pl.BlockSpec((1, d), const),
            pl.BlockSpec((None, d, tf), up_chunk),
            pl.BlockSpec((None, d, tf), up_chunk),
            pl.BlockSpec((tf, d), down_chunk),
            pl.BlockSpec((1, d), const),
        ],
        out_specs=pl.BlockSpec((tm, d), row),
        out_shape=jax.ShapeDtypeStruct((m, d), jnp.float32),
        scratch_shapes=[
            pltpu.VMEM((tm, d), jnp.bfloat16),
            pltpu.VMEM((2, tm, tf), jnp.bfloat16),
        ],
        compiler_params=pltpu.CompilerParams(
            dimension_semantics=("parallel", "arbitrary"),
            vmem_limit_bytes=VMEM_LIMIT_BYTES),
    )(x2d, pre_g.reshape(1, d), _ffn_up_weight(w_gate, tf), _ffn_up_weight(w_up, tf),
      w_down.astype(jnp.bfloat16), post_g.reshape(1, d))


def _shift_rows(cur, tail, k):
    rolled = pltpu.roll(cur, k, axis=0)
    tail_rolled = pltpu.roll(tail, k, axis=0)
    row = lax.broadcasted_iota(jnp.int32, tail.shape, 0)
    head = jnp.where(row < k, tail_rolled, rolled[:SUBLANES])
    return jnp.concatenate([head, rolled[SUBLANES:]], axis=0)


def _causal_conv(cur, tail, w_ref):
    k_w = w_ref.shape[0]
    y = _shift_rows(cur, tail, k_w - 1) * w_ref[0:1, :]
    for k in range(1, k_w - 1):
        y = y + _shift_rows(cur, tail, k_w - 1 - k) * w_ref[k:k + 1, :]
    return y + cur * w_ref[k_w - 1:k_w, :]


def _linear_scan(a, b, h0):
    t = a.shape[0]
    row = lax.broadcasted_iota(jnp.int32, a.shape, 0)
    s = 1
    while s < t:
        keep = row >= s
        b = jnp.where(keep, a * pltpu.roll(b, s, axis=0) + b, b)
        a = jnp.where(keep, a * pltpu.roll(a, s, axis=0), a)
        s *= 2
    return a * h0 + b


def _gelu_tanh(x):
    c = math.sqrt(2.0 / math.pi)
    return x * (0.5 * (1.0 + jnp.tanh(c * (x + 0.044715 * (x * x * x)))))


def _mixer_kernel(h_ref, pre_g_ref, w_in_ref, lru_cw_ref, lru_cb_ref, w_ri_ref, b_r_ref, b_i_ref,
                  lam_ref, sc_cw_ref, w_out_ref, post_g_ref, o_ref,
                  lru_tail_ref, sc_tail_ref, h_carry_ref, y_ref):
    d_lru = lam_ref.shape[-1]
    d_sc = sc_cw_ref.shape[-1]
    n_heads, head_dim, _ = w_ri_ref.shape
    t = h_ref.shape[0]

    @pl.when(pl.program_id(1) == 0)
    def _():
        lru_tail_ref[...] = jnp.zeros_like(lru_tail_ref)
        sc_tail_ref[...] = jnp.zeros_like(sc_tail_ref)
        h_carry_ref[...] = jnp.zeros_like(h_carry_ref)

    x = h_ref[...]
    u = (x * _rms_scale(x) * pre_g_ref[...]).astype(jnp.bfloat16)

    def proj(col0, width):
        return jnp.dot(u, w_in_ref[:, col0:col0 + width], preferred_element_type=jnp.float32)

    lru_x = proj(0, d_lru)
    xc = _causal_conv(lru_x, lru_tail_ref[...], lru_cw_ref) + lru_cb_ref[...]
    lru_tail_ref[...] = lru_x[t - SUBLANES:, :]

    neg_lam = -lam_ref[...]
    softplus_neg_lam = jnp.maximum(neg_lam, 0.0) + jnp.log1p(jnp.exp(-jnp.abs(neg_lam)))
    r_parts, i_parts = [], []
    for hd in range(n_heads):
        cols = slice(hd * head_dim, (hd + 1) * head_dim)
        ri = jnp.dot(xc[:, cols].astype(jnp.bfloat16), w_ri_ref[hd],
                     preferred_element_type=jnp.float32)
        r_parts.append(jax.nn.sigmoid(ri[:, :head_dim] + b_r_ref[:, cols]))
        i_parts.append(jax.nn.sigmoid(ri[:, head_dim:] + b_i_ref[:, cols]))
    r = jnp.concatenate(r_parts, axis=-1)
    i = jnp.concatenate(i_parts, axis=-1)
    log_a = (-RG_C) * r * softplus_neg_lam
    a = jnp.exp(log_a)
    th = jnp.tanh(log_a)
    b = jnp.sqrt((-2.0 * th) / (1.0 - th)) * (i * xc)
    h_seq = _linear_scan(a, b, h_carry_ref[SUBLANES - 1:SUBLANES, :])
    h_carry_ref[...] = h_seq[t - SUBLANES:, :]
    lru_gate = proj(d_lru, d_lru)
    y_ref[:, :d_lru] = (h_seq * _gelu_tanh(lru_gate)).astype(y_ref.dtype)

    sc_b = proj(2 * d_lru, d_sc)
    sc_c = proj(2 * d_lru + d_sc, d_sc)
    sc_x = proj(2 * d_lru + 2 * d_sc, d_sc)
    v = sc_c * sc_x
    y_sc = sc_b * _causal_conv(v, sc_tail_ref[...], sc_cw_ref)
    sc_tail_ref[...] = v[t - SUBLANES:, :]
    y_ref[:, d_lru:] = y_sc.astype(y_ref.dtype)

    y = jnp.dot(y_ref[...], w_out_ref[...], preferred_element_type=jnp.float32)
    o_ref[...] = x + y * _rms_scale(y) * post_g_ref[...]


def _mixer(h, pre_g, w_in, lru_conv_w, lru_conv_b, w_ri, b_r, b_i, lam, sc_conv_w, w_out, post_g):
    bsz, seq, d = h.shape
    d_in = w_in.shape[1]
    d_lru = lam.shape[-1]
    d_sc = sc_conv_w.shape[-1]
    d_mix = d_lru + d_sc
    t = min(MIX_SEQ_TILE, seq)
    assert seq % t == 0 and t % SUBLANES == 0
    tile = lambda b, s: (b, s, 0)
    const2 = lambda b, s: (0, 0)
    const3 = lambda b, s: (0, 0, 0)
    resident = dict(pipeline_mode=pl.Buffered(1))
    return pl.pallas_call(
        _mixer_kernel,
        name="mixer",
        grid=(bsz, seq // t),
        in_specs=[
            pl.BlockSpec((None, t, d), tile),
            pl.BlockSpec((1, d), const2),
            pl.BlockSpec((d, d_in), const2, **resident),
            pl.BlockSpec(lru_conv_w.shape, const2),
            pl.BlockSpec((1, d_lru), const2),
            pl.BlockSpec(w_ri.shape, const3, **resident),
            pl.BlockSpec((1, d_lru), const2),
            pl.BlockSpec((1, d_lru), const2),
            pl.BlockSpec((1, d_lru), const2),
            pl.BlockSpec(sc_conv_w.shape, const2),
            pl.BlockSpec((d_mix, d), const2, **resident),
            pl.BlockSpec((1, d), const2),
        ],
        out_specs=pl.BlockSpec((None, t, d), tile),
        out_shape=jax.ShapeDtypeStruct((bsz, seq, d), jnp.float32),
        scratch_shapes=[
            pltpu.VMEM((SUBLANES, d_lru), jnp.float32),
            pltpu.VMEM((SUBLANES, d_sc), jnp.float32),
            pltpu.VMEM((SUBLANES, d_lru), jnp.float32),
            pltpu.VMEM((t, d_mix), jnp.bfloat16),
        ],
        compiler_params=pltpu.CompilerParams(
            dimension_semantics=("parallel", "arbitrary"),
            vmem_limit_bytes=VMEM_LIMIT_BYTES),
    )(h, pre_g.reshape(1, d), w_in, lru_conv_w, lru_conv_b.reshape(1, d_lru), w_ri,
      b_r.reshape(1, d_lru), b_i.reshape(1, d_lru), lam.reshape(1, d_lru), sc_conv_w, w_out,
      post_g.reshape(1, d))


def kernel(x, ffn1_pre_g, ffn1_w_gate, ffn1_w_up, ffn1_w_down, ffn1_post_g, mix_pre_g, w_in, lru_conv_w, lru_conv_b, lru_w_r, lru_b_r, lru_w_i, lru_b_i, lru_lambda, sc_conv_w, w_out, mix_post_g, ffn2_pre_g, ffn2_w_gate, ffn2_w_up, ffn2_w_down, ffn2_post_g):
    bsz, seq, d = x.shape
    bf16 = lambda w: w.astype(jnp.bfloat16)
    h = x
    for l in range(ffn1_pre_g.shape[0]):
        h2d = _ffn(h.reshape(bsz * seq, d), ffn1_pre_g[l], ffn1_w_gate[l], ffn1_w_up[l],
                   ffn1_w_down[l], ffn1_post_g[l])
        w_ri = bf16(jnp.concatenate([lru_w_r[l], lru_w_i[l]], axis=-1))
        h = _mixer(h2d.reshape(bsz, seq, d), mix_pre_g[l], bf16(w_in[l]), lru_conv_w[l],
                   lru_conv_b[l], w_ri, lru_b_r[l], lru_b_i[l], lru_lambda[l], sc_conv_w[l],
                   bf16(w_out[l]), mix_post_g[l])
        h2d = _ffn(h.reshape(bsz * seq, d), ffn2_pre_g[l], ffn2_w_gate[l], ffn2_w_up[l],
                   ffn2_w_down[l], ffn2_post_g[l])
        h = h2d.reshape(bsz, seq, d)
    return h
```

```python
import functools
import math

import jax
import jax.numpy as jnp
from jax import lax
from jax.experimental import pallas as pl
from jax.experimental.pallas import tpu as pltpu

EPS = 1e-6
RG_C = 8.0
SUBLANES = 8
VMEM_LIMIT_BYTES = 60 * 1024 * 1024

FFN_ROW_TILE = 1024
FFN_ROW_SUBTILE = 512
FFN_EDGE_ROW_SUBTILE = 256
FFN_FF_TILE = 512
FFN_OUT_COL_TILE = 512
MIX_SEQ_TILE = 256


def _rms_scale(x):
    return lax.rsqrt(jnp.mean(x * x, axis=-1, keepdims=True) + EPS)


def _ffn_kernel(x_ref, pre_g_ref, wg_ref, wu_ref, wd_ref, post_g_ref, o_ref, xn_ref, hid_ref,
                *, row_sub, edge_sub, out_col_tile):
    j = pl.program_id(1)
    n_ff = pl.num_programs(1) - 1
    d_model = o_ref.shape[-1]

    row_subs = [slice(r, r + row_sub) for r in range(0, o_ref.shape[0], row_sub)]

    def hidden_rows(slot, rows):
        xn = xn_ref[rows, :]
        gate = jnp.dot(xn, wg_ref[...], preferred_element_type=jnp.float32)
        up = jnp.dot(xn, wu_ref[...], preferred_element_type=jnp.float32)
        hid_ref[slot, rows, :] = ((gate * jax.nn.sigmoid(gate)) * up).astype(hid_ref.dtype)

    def down_rows(slot, rows):
        hid = hid_ref[slot, rows, :]
        for c in range(0, d_model, out_col_tile):
            cols = slice(c, c + out_col_tile)
            o_ref[rows, cols] += jnp.dot(hid, wd_ref[:, cols], preferred_element_type=jnp.float32)

    slot = lax.rem(j, 2)

    edge_subs = [slice(r, r + edge_sub) for r in range(0, o_ref.shape[0], edge_sub)]

    def pre_norm(rows):
        x = x_ref[rows, :]
        xn_ref[rows, :] = (x * _rms_scale(x) * pre_g_ref[...]).astype(xn_ref.dtype)

    def post_norm(rows):
        f = o_ref[rows, :]
        o_ref[rows, :] = x_ref[rows, :] + 0.5 * (f * _rms_scale(f) * post_g_ref[...])

    @pl.when(j == 0)
    def _():
        pre_norm(edge_subs[0])
        for k, rows in enumerate(edge_subs):
            if k + 1 < len(edge_subs):
                pre_norm(edge_subs[k + 1])
            o_ref[rows, :] = jnp.zeros((edge_sub, d_model), o_ref.dtype)
            hidden_rows(slot, rows)

    @pl.when(jnp.logical_and(j > 0, j < n_ff))
    def _():
        for rows in row_subs:
            down_rows(1 - slot, rows)
        for rows in row_subs:
            hidden_rows(slot, rows)

    @pl.when(j == n_ff)
    def _():
        down_rows(1 - slot, edge_subs[0])
        for k, rows in enumerate(edge_subs):
            if k + 1 < len(edge_subs):
                down_rows(1 - slot, edge_subs[k + 1])
            post_norm(rows)


def _cast_chunks_kernel(w_ref, o_ref):
    o_ref[...] = w_ref[...].astype(o_ref.dtype)


def _ffn_up_weight(w, tf):
    d, f = w.shape
    return pl.pallas_call(
        _cast_chunks_kernel,
        name="cast_chunks",
        grid=(f // tf,),
        in_specs=[pl.BlockSpec((d, tf), lambda j: (0, j))],
        out_specs=pl.BlockSpec((None, d, tf), lambda j: (j, 0, 0)),
        out_shape=jax.ShapeDtypeStruct((f // tf, d, tf), jnp.bfloat16),
        compiler_params=pltpu.CompilerParams(dimension_semantics=("parallel",)),
    )(w)


def _ffn(x2d, pre_g, w_gate, w_up, w_down, post_g):
    m, d = x2d.shape
    f = w_gate.shape[1]
    tm, tf = min(FFN_ROW_TILE, m), min(FFN_FF_TILE, f)
    assert m % tm == 0 and f % tf == 0
    n_ff = f // tf
    row_sub = min(FFN_ROW_SUBTILE, tm)
    edge_sub = min(FFN_EDGE_ROW_SUBTILE, tm)
    out_col_tile = min(FFN_OUT_COL_TILE, d)
    assert tm % row_sub == 0 and tm % edge_sub == 0 and d % out_col_tile == 0
    row = lambda i, j: (i, 0)
    const = lambda i, j: (0, 0)
    up_chunk = lambda i, j: (jnp.minimum(j, n_ff - 1), 0, 0)
    down_chunk = lambda i, j: (jnp.maximum(j - 1, 0), 0)
    return pl.pallas_call(
        functools.partial(_ffn_kernel, row_sub=row_sub, edge_sub=edge_sub,
                          out_col_tile=out_col_tile),
        name="ffn",
        grid=(m // tm, n_ff + 1),
        in_specs=[
            pl.BlockSpec((tm, d), row),
            pl.BlockSpec((1, d), const),
            pl.BlockSpec((None, d, tf), up_chunk),
            pl.BlockSpec((None, d, tf), up_chunk),
            pl.BlockSpec((tf, d), down_chunk),
            pl.BlockSpec((1, d), const),
        ],
        out_specs=pl.BlockSpec((tm, d), row),
        out_shape=jax.ShapeDtypeStruct((m, d), jnp.float32),
        scratch_shapes=[
            pltpu.VMEM((tm, d), jnp.bfloat16),
            pltpu.VMEM((2, tm, tf), jnp.bfloat16),
        ],
        compiler_params=pltpu.CompilerParams(
            dimension_semantics=("parallel", "arbitrary"),
            vmem_limit_bytes=VMEM_LIMIT_BYTES),
    )(x2d, pre_g.reshape(1, d), _ffn_up_weight(w_gate, tf), _ffn_up_weight(w_up, tf),
      w_down.astype(jnp.bfloat16), post_g.reshape(1, d))


def _row_groups(x):
    t, c = x.shape
    return x.reshape(t // SUBLANES, SUBLANES, c)


def _shift_rows(cur, tail, k):
    rolled = pltpu.roll(cur, k, axis=1)
    tail_rolled = pltpu.roll(tail, k, axis=0)
    prev = jnp.concatenate([tail_rolled[None], rolled[:-1]], axis=0)
    row = lax.broadcasted_iota(jnp.int32, (1,) + tail.shape, 1)
    return jnp.where(row >= k, rolled, prev)


def _causal_conv(cur, tail, w_ref):
    k_w = w_ref.shape[0]
    y = _shift_rows(cur, tail, k_w - 1) * w_ref[0:1, :]
    for k in range(1, k_w - 1):
        y = y + _shift_rows(cur, tail, k_w - 1 - k) * w_ref[k:k + 1, :]
    return y + cur * w_ref[k_w - 1:k_w, :]


def _linear_scan(a, b, h0):
    row = lax.broadcasted_iota(jnp.int32, (1,) + a.shape[1:], 1)
    s = 1
    while s < SUBLANES:
        keep = row >= s
        b = jnp.where(keep, a * pltpu.roll(b, s, axis=1) + b, b)
        a = jnp.where(keep, a * pltpu.roll(a, s, axis=1), a)
        s *= 2
    groups = []
    carry = h0
    for g in range(a.shape[0]):
        h_g = a[g] * carry + b[g]
        groups.append(h_g)
        carry = h_g[SUBLANES - 1:SUBLANES, :]
    return jnp.concatenate(groups, axis=0)


def _gelu_tanh(x):
    c = math.sqrt(2.0 / math.pi)
    return x * (0.5 * (1.0 + jnp.tanh(c * (x + 0.044715 * (x * x * x)))))


def _mixer_kernel(h_ref, pre_g_ref, w_in_ref, lru_cw_ref, lru_cb_ref, w_ri_ref, b_r_ref, b_i_ref,
                  lam_ref, sc_cw_ref, w_out_ref, post_g_ref, o_ref,
                  lru_tail_ref, sc_tail_ref, h_carry_ref, y_ref):
    d_lru = lam_ref.shape[-1]
    d_sc = sc_cw_ref.shape[-1]
    n_heads, head_dim, _ = w_ri_ref.shape
    t = h_ref.shape[0]

    @pl.when(pl.program_id(1) == 0)
    def _():
        lru_tail_ref[...] = jnp.zeros_like(lru_tail_ref)
        sc_tail_ref[...] = jnp.zeros_like(sc_tail_ref)
        h_carry_ref[...] = jnp.zeros_like(h_carry_ref)

    x = h_ref[...]
    u = (x * _rms_scale(x) * pre_g_ref[...]).astype(jnp.bfloat16)

    def proj(col0, width):
        return jnp.dot(u, w_in_ref[:, col0:col0 + width], preferred_element_type=jnp.float32)

    lru_x = proj(0, d_lru)

    sc_cx = proj(2 * d_lru + d_sc, 2 * d_sc)
    v = sc_cx[:, :d_sc] * sc_cx[:, d_sc:]
    sc_conv = _causal_conv(_row_groups(v), sc_tail_ref[...], sc_cw_ref).reshape(t, d_sc)
    sc_tail_ref[...] = v[t - SUBLANES:, :]

    xc = _causal_conv(_row_groups(lru_x), lru_tail_ref[...], lru_cw_ref) + lru_cb_ref[...]
    xc = xc.reshape(t, d_lru)
    lru_tail_ref[...] = lru_x[t - SUBLANES:, :]

    neg_lam = -lam_ref[...]
    softplus_neg_lam = jnp.maximum(neg_lam, 0.0) + jnp.log1p(jnp.exp(-jnp.abs(neg_lam)))
    r_parts, i_parts = [], []
    for hd in range(n_heads):
        cols = slice(hd * head_dim, (hd + 1) * head_dim)
        ri = jnp.dot(xc[:, cols].astype(jnp.bfloat16), w_ri_ref[hd],
                     preferred_element_type=jnp.float32)
        r_parts.append(jax.nn.sigmoid(ri[:, :head_dim] + b_r_ref[:, cols]))
        i_parts.append(jax.nn.sigmoid(ri[:, head_dim:] + b_i_ref[:, cols]))
    gate_b = proj(d_lru, d_lru + d_sc)
    r = jnp.concatenate(r_parts, axis=-1)
    i = jnp.concatenate(i_parts, axis=-1)
    log_a = (-RG_C) * r * softplus_neg_lam
    a = jnp.exp(log_a)
    th = jnp.tanh(log_a)
    b = jnp.sqrt((-2.0 * th) / (1.0 - th)) * (i * xc)
    h_seq = _linear_scan(_row_groups(a), _row_groups(b), h_carry_ref[SUBLANES - 1:SUBLANES, :])
    h_carry_ref[...] = h_seq[t - SUBLANES:, :]
    y_ref[:, :d_lru] = (h_seq * _gelu_tanh(gate_b[:, :d_lru])).astype(y_ref.dtype)

    y_ref[:, d_lru:] = (gate_b[:, d_lru:] * sc_conv).astype(y_ref.dtype)

    y = jnp.dot(y_ref[...], w_out_ref[...], preferred_element_type=jnp.float32)
    o_ref[...] = x + y * _rms_scale(y) * post_g_ref[...]


def _mixer(h, pre_g, w_in, lru_conv_w, lru_conv_b, w_ri, b_r, b_i, lam, sc_conv_w, w_out, post_g):
    bsz, seq, d = h.shape
    d_in = w_in.shape[1]
    d_lru = lam.shape[-1]
    d_sc = sc_conv_w.shape[-1]
    d_mix = d_lru + d_sc
    t = min(MIX_SEQ_TILE, seq)
    assert seq % t == 0 and t % SUBLANES == 0
    tile = lambda b, s: (b, s, 0)
    const2 = lambda b, s: (0, 0)
    const3 = lambda b, s: (0, 0, 0)
    resident = dict(pipeline_mode=pl.Buffered(1))
    return pl.pallas_call(
        _mixer_kernel,
        name="mixer",
        grid=(bsz, seq // t),
        in_specs=[
            pl.BlockSpec((None, t, d), tile),
            pl.BlockSpec((1, d), const2),
            pl.BlockSpec((d, d_in), const2, **resident),
            pl.BlockSpec(lru_conv_w.shape, const2),
            pl.BlockSpec((1, d_lru), const2),
            pl.BlockSpec(w_ri.shape, const3, **resident),
            pl.BlockSpec((1, d_lru), const2),
            pl.BlockSpec((1, d_lru), const2),
            pl.BlockSpec((1, d_lru), const2),
            pl.BlockSpec(sc_conv_w.shape, const2),
            pl.BlockSpec((d_mix, d), const2, **resident),
            pl.BlockSpec((1, d), const2),
        ],
        out_specs=pl.BlockSpec((None, t, d), tile),
        out_shape=jax.ShapeDtypeStruct((bsz, seq, d), jnp.float32),
        scratch_shapes=[
            pltpu.VMEM((SUBLANES, d_lru), jnp.float32),
            pltpu.VMEM((SUBLANES, d_sc), jnp.float32),
            pltpu.VMEM((SUBLANES, d_lru), jnp.float32),
            pltpu.VMEM((t, d_mix), jnp.bfloat16),
        ],
        compiler_params=pltpu.CompilerParams(
            dimension_semantics=("parallel", "arbitrary"),
            vmem_limit_bytes=VMEM_LIMIT_BYTES),
    )(h, pre_g.reshape(1, d), w_in, lru_conv_w, lru_conv_b.reshape(1, d_lru), w_ri,
      b_r.reshape(1, d_lru), b_i.reshape(1, d_lru), lam.reshape(1, d_lru), sc_conv_w, w_out,
      post_g.reshape(1, d))


def kernel(x, ffn1_pre_g, ffn1_w_gate, ffn1_w_up, ffn1_w_down, ffn1_post_g, mix_pre_g, w_in, lru_conv_w, lru_conv_b, lru_w_r, lru_b_r, lru_w_i, lru_b_i, lru_lambda, sc_conv_w, w_out, mix_post_g, ffn2_pre_g, ffn2_w_gate, ffn2_w_up, ffn2_w_down, ffn2_post_g):
    bsz, seq, d = x.shape
    bf16 = lambda w: w.astype(jnp.bfloat16)
    h = x
    for l in range(ffn1_pre_g.shape[0]):
        h2d = _ffn(h.reshape(bsz * seq, d), ffn1_pre_g[l], ffn1_w_gate[l], ffn1_w_up[l],
                   ffn1_w_down[l], ffn1_post_g[l])
        w_ri = bf16(jnp.concatenate([lru_w_r[l], lru_w_i[l]], axis=-1))
        h = _mixer(h2d.reshape(bsz, seq, d), mix_pre_g[l], bf16(w_in[l]), lru_conv_w[l],
                   lru_conv_b[l], w_ri, lru_b_r[l], lru_b_i[l], lru_lambda[l], sc_conv_w[l],
                   bf16(w_out[l]), mix_post_g[l])
        h2d = _ffn(h.reshape(bsz * seq, d), ffn2_pre_g[l], ffn2_w_gate[l], ffn2_w_up[l],
                   ffn2_w_down[l], ffn2_post_g[l])
        h = h2d.reshape(bsz, seq, d)
    return h
```

```python
import functools
import math

import jax
import jax.numpy as jnp
from jax import lax
from jax.experimental import pallas as pl
from jax.experimental.pallas import tpu as pltpu

EPS = 1e-6
RG_C = 8.0
SUBLANES = 8
VMEM_LIMIT_BYTES = 60 * 1024 * 1024

FFN_ROW_TILE = 1024
FFN_ROW_SUBTILE = 512
FFN_EDGE_ROW_SUBTILE = 256
FFN_FF_TILE = 512
FFN_OUT_COL_TILE = 512
MIX_SEQ_TILE = 256
PACK_COL_TILE = 512


def _rms_scale(x):
    return lax.rsqrt(jnp.mean(x * x, axis=-1, keepdims=True) + EPS)


def _ffn_kernel(x_ref, pre_g_ref, wg_ref, wu_ref, wd_ref, post_g_ref, o_ref, xn_ref, hid_ref,
                *, row_sub, edge_sub, out_col_tile):
    j = pl.program_id(1)
    n_ff = pl.num_programs(1) - 1
    d_model = o_ref.shape[-1]

    row_subs = [slice(r, r + row_sub) for r in range(0, o_ref.shape[0], row_sub)]

    def hidden_rows(slot, rows):
        xn = xn_ref[rows, :]
        gate = jnp.dot(xn, wg_ref[...], preferred_element_type=jnp.float32)
        up = jnp.dot(xn, wu_ref[...], preferred_element_type=jnp.float32)
        hid_ref[slot, rows, :] = ((gate * jax.nn.sigmoid(gate)) * up).astype(hid_ref.dtype)

    def down_rows(slot, rows):
        hid = hid_ref[slot, rows, :]
        for c in range(0, d_model, out_col_tile):
            cols = slice(c, c + out_col_tile)
            o_ref[rows, cols] += jnp.dot(hid, wd_ref[:, cols], preferred_element_type=jnp.float32)

    slot = lax.rem(j, 2)

    edge_subs = [slice(r, r + edge_sub) for r in range(0, o_ref.shape[0], edge_sub)]

    def pre_norm(rows):
        x = x_ref[rows, :]
        xn_ref[rows, :] = (x * _rms_scale(x) * pre_g_ref[...]).astype(xn_ref.dtype)

    def post_norm(rows):
        f = o_ref[rows, :]
        o_ref[rows, :] = x_ref[rows, :] + 0.5 * (f * _rms_scale(f) * post_g_ref[...])

    @pl.when(j == 0)
    def _():
        pre_norm(edge_subs[0])
        for k, rows in enumerate(edge_subs):
            if k + 1 < len(edge_subs):
                pre_norm(edge_subs[k + 1])
            o_ref[rows, :] = jnp.zeros((edge_sub, d_model), o_ref.dtype)
            hidden_rows(slot, rows)

    @pl.when(jnp.logical_and(j > 0, j < n_ff))
    def _():
        for rows in row_subs:
            down_rows(1 - slot, rows)
        for rows in row_subs:
            hidden_rows(slot, rows)

    @pl.when(j == n_ff)
    def _():
        down_rows(1 - slot, edge_subs[0])
        for k, rows in enumerate(edge_subs):
            if k + 1 < len(edge_subs):
                down_rows(1 - slot, edge_subs[k + 1])
            post_norm(rows)


def _cast_chunks_kernel(w_ref, o_ref):
    o_ref[...] = w_ref[...].astype(o_ref.dtype)


def _ffn_up_weight(w, tf):
    d, f = w.shape
    return pl.pallas_call(
        _cast_chunks_kernel,
        name="cast_chunks",
        grid=(f // tf,),
        in_specs=[pl.BlockSpec((d, tf), lambda j: (0, j))],
        out_specs=pl.BlockSpec((None, d, tf), lambda j: (j, 0, 0)),
        out_shape=jax.ShapeDtypeStruct((f // tf, d, tf), jnp.bfloat16),
        compiler_params=pltpu.CompilerParams(dimension_semantics=("parallel",)),
    )(w)


def _pack_rows_kernel(w_ref, o_ref):
    o_ref[...] = pltpu.bitcast(w_ref[...].astype(jnp.bfloat16), o_ref.dtype)


def _pack_bf16_rows(w, col_tile):
    k, n = w.shape
    tn = min(col_tile, n)
    assert n % tn == 0 and k % (2 * SUBLANES) == 0
    return pl.pallas_call(
        _pack_rows_kernel,
        name="pack_rows",
        grid=(n // tn,),
        in_specs=[pl.BlockSpec((k, tn), lambda j: (0, j))],
        out_specs=pl.BlockSpec((k // 2, tn), lambda j: (0, j)),
        out_shape=jax.ShapeDtypeStruct((k // 2, n), jnp.uint32),
        compiler_params=pltpu.CompilerParams(dimension_semantics=("parallel",)),
    )(w)


def _ffn(x2d, pre_g, w_gate, w_up, w_down, post_g):
    m, d = x2d.shape
    f = w_gate.shape[1]
    tm, tf = min(FFN_ROW_TILE, m), min(FFN_FF_TILE, f)
    assert m % tm == 0 and f % tf == 0
    n_ff = f // tf
    row_sub = min(FFN_ROW_SUBTILE, tm)
    edge_sub = min(FFN_EDGE_ROW_SUBTILE, tm)
    out_col_tile = min(FFN_OUT_COL_TILE, d)
    assert tm % row_sub == 0 and tm % edge_sub == 0 and d % out_col_tile == 0
    row = lambda i, j: (i, 0)
    const = lambda i, j: (0, 0)
    up_chunk = lambda i, j: (jnp.minimum(j, n_ff - 1), 0, 0)
    down_chunk = lambda i, j: (jnp.maximum(j - 1, 0), 0)
    return pl.pallas_call(
        functools.partial(_ffn_kernel, row_sub=row_sub, edge_sub=edge_sub,
                          out_col_tile=out_col_tile),
        name="ffn",
        grid=(m // tm, n_ff + 1),
        in_specs=[
            pl.BlockSpec((tm, d), row),
            pl.BlockSpec((1, d), const),
            pl.BlockSpec((None, d, tf), up_chunk),
            pl.BlockSpec((None, d, tf), up_chunk),
            pl.BlockSpec((tf, d), down_chunk),
            pl.BlockSpec((1, d), const),
        ],
        out_specs=pl.BlockSpec((tm, d), row),
        out_shape=jax.ShapeDtypeStruct((m, d), jnp.float32),
        scratch_shapes=[
            pltpu.VMEM((tm, d), jnp.bfloat16),
            pltpu.VMEM((2, tm, tf), jnp.bfloat16),
        ],
        compiler_params=pltpu.CompilerParams(
            dimension_semantics=("parallel", "arbitrary"),
            vmem_limit_bytes=VMEM_LIMIT_BYTES),
    )(x2d, pre_g.reshape(1, d), _ffn_up_weight(w_gate, tf), _ffn_up_weight(w_up, tf),
      w_down.astype(jnp.bfloat16), post_g.reshape(1, d))


def _row_groups(x):
    t, c = x.shape
    return x.reshape(t // SUBLANES, SUBLANES, c)


def _shift_rows(cur, tail, k):
    rolled = pltpu.roll(cur, k, axis=1)
    tail_rolled = pltpu.roll(tail, k, axis=0)
    prev = jnp.concatenate([tail_rolled[None], rolled[:-1]], axis=0)
    row = lax.broadcasted_iota(jnp.int32, (1,) + tail.shape, 1)
    return jnp.where(row >= k, rolled, prev)


def _causal_conv(cur, tail, w_ref):
    k_w = w_ref.shape[0]
    y = _shift_rows(cur, tail, k_w - 1) * w_ref[0:1, :]
    for k in range(1, k_w - 1):
        y = y + _shift_rows(cur, tail, k_w - 1 - k) * w_ref[k:k + 1, :]
    return y + cur * w_ref[k_w - 1:k_w, :]


def _linear_scan(a, b, h0):
    row = lax.broadcasted_iota(jnp.int32, (1,) + a.shape[1:], 1)
    s = 1
    while s < SUBLANES:
        keep = row >= s
        b = jnp.where(keep, a * pltpu.roll(b, s, axis=1) + b, b)
        a = jnp.where(keep, a * pltpu.roll(a, s, axis=1), a)
        s *= 2
    groups = []
    carry = h0
    for g in range(a.shape[0]):
        h_g = a[g] * carry + b[g]
        groups.append(h_g)
        carry = h_g[SUBLANES - 1:SUBLANES, :]
    return jnp.concatenate(groups, axis=0)


def _gelu_tanh(x):
    c = math.sqrt(2.0 / math.pi)
    return x * (0.5 * (1.0 + jnp.tanh(c * (x + 0.044715 * (x * x * x)))))


def _mixer_kernel(h_ref, pre_g_ref, w_in_ref, lru_cw_ref, lru_cb_ref, w_ri_ref, b_r_ref, b_i_ref,
                  lam_ref, sc_cw_ref, w_out_ref, post_g_ref, o_ref,
                  lru_tail_ref, sc_tail_ref, h_carry_ref, y_ref):
    d_lru = lam_ref.shape[-1]
    d_sc = sc_cw_ref.shape[-1]
    n_heads, head_dim, _ = w_ri_ref.shape
    t = h_ref.shape[0]

    @pl.when(pl.program_id(1) == 0)
    def _():
        lru_tail_ref[...] = jnp.zeros_like(lru_tail_ref)
        sc_tail_ref[...] = jnp.zeros_like(sc_tail_ref)
        h_carry_ref[...] = jnp.zeros_like(h_carry_ref)

    x = h_ref[...]
    u = (x * _rms_scale(x) * pre_g_ref[...]).astype(jnp.bfloat16)

    def proj(col0, width):
        w = pltpu.bitcast(w_in_ref[:, col0:col0 + width], jnp.bfloat16)
        return jnp.dot(u, w, preferred_element_type=jnp.float32)

    lru_x = proj(0, d_lru)

    sc_cx = proj(2 * d_lru + d_sc, 2 * d_sc)
    v = sc_cx[:, :d_sc] * sc_cx[:, d_sc:]
    sc_conv = _causal_conv(_row_groups(v), sc_tail_ref[...], sc_cw_ref).reshape(t, d_sc)
    sc_tail_ref[...] = v[t - SUBLANES:, :]

    xc = _causal_conv(_row_groups(lru_x), lru_tail_ref[...], lru_cw_ref) + lru_cb_ref[...]
    xc = xc.reshape(t, d_lru)
    lru_tail_ref[...] = lru_x[t - SUBLANES:, :]

    neg_lam = -lam_ref[...]
    softplus_neg_lam = jnp.maximum(neg_lam, 0.0) + jnp.log1p(jnp.exp(-jnp.abs(neg_lam)))
    r_parts, i_parts = [], []
    for hd in range(n_heads):
        cols = slice(hd * head_dim, (hd + 1) * head_dim)
        ri = jnp.dot(xc[:, cols].astype(jnp.bfloat16), w_ri_ref[hd],
                     preferred_element_type=jnp.float32)
        r_parts.append(jax.nn.sigmoid(ri[:, :head_dim] + b_r_ref[:, cols]))
        i_parts.append(jax.nn.sigmoid(ri[:, head_dim:] + b_i_ref[:, cols]))
    gate_b = proj(d_lru, d_lru + d_sc)
    r = jnp.concatenate(r_parts, axis=-1)
    i = jnp.concatenate(i_parts, axis=-1)
    log_a = (-RG_C) * r * softplus_neg_lam
    a = jnp.exp(log_a)
    b = jnp.sqrt(-jnp.tanh(log_a) * (a * a + 1.0)) * (i * xc)
    h_seq = _linear_scan(_row_groups(a), _row_groups(b), h_carry_ref[SUBLANES - 1:SUBLANES, :])
    h_carry_ref[...] = h_seq[t - SUBLANES:, :]
    y_ref[:, :d_lru] = (h_seq * _gelu_tanh(gate_b[:, :d_lru])).astype(y_ref.dtype)

    y_ref[:, d_lru:] = (gate_b[:, d_lru:] * sc_conv).astype(y_ref.dtype)

    y = jnp.dot(y_ref[...], pltpu.bitcast(w_out_ref[...], jnp.bfloat16),
                preferred_element_type=jnp.float32)
    o_ref[...] = x + y * _rms_scale(y) * post_g_ref[...]


def _mixer(h, pre_g, w_in, lru_conv_w, lru_conv_b, w_ri, b_r, b_i, lam, sc_conv_w, w_out, post_g):
    bsz, seq, d = h.shape
    d_in = w_in.shape[1]
    d_lru = lam.shape[-1]
    d_sc = sc_conv_w.shape[-1]
    d_mix = d_lru + d_sc
    t = min(MIX_SEQ_TILE, seq)
    assert seq % t == 0 and t % SUBLANES == 0
    tile = lambda b, s: (b, s, 0)
    const2 = lambda b, s: (0, 0)
    const3 = lambda b, s: (0, 0, 0)
    resident = dict(pipeline_mode=pl.Buffered(1))
    return pl.pallas_call(
        _mixer_kernel,
        name="mixer",
        grid=(bsz, seq // t),
        in_specs=[
            pl.BlockSpec((None, t, d), tile),
            pl.BlockSpec((1, d), const2),
            pl.BlockSpec(w_in.shape, const2, **resident),
            pl.BlockSpec(lru_conv_w.shape, const2),
            pl.BlockSpec((1, d_lru), const2),
            pl.BlockSpec(w_ri.shape, const3, **resident),
            pl.BlockSpec((1, d_lru), const2),
            pl.BlockSpec((1, d_lru), const2),
            pl.BlockSpec((1, d_lru), const2),
            pl.BlockSpec(sc_conv_w.shape, const2),
            pl.BlockSpec(w_out.shape, const2, **resident),
            pl.BlockSpec((1, d), const2),
        ],
        out_specs=pl.BlockSpec((None, t, d), tile),
        out_shape=jax.ShapeDtypeStruct((bsz, seq, d), jnp.float32),
        scratch_shapes=[
            pltpu.VMEM((SUBLANES, d_lru), jnp.float32),
            pltpu.VMEM((SUBLANES, d_sc), jnp.float32),
            pltpu.VMEM((SUBLANES, d_lru), jnp.float32),
            pltpu.VMEM((t, d_mix), jnp.bfloat16),
        ],
        compiler_params=pltpu.CompilerParams(
            dimension_semantics=("parallel", "arbitrary"),
            vmem_limit_bytes=VMEM_LIMIT_BYTES),
    )(h, pre_g.reshape(1, d), w_in, lru_conv_w, lru_conv_b.reshape(1, d_lru), w_ri,
      b_r.reshape(1, d_lru), b_i.reshape(1, d_lru), lam.reshape(1, d_lru), sc_conv_w, w_out,
      post_g.reshape(1, d))


def kernel(x, ffn1_pre_g, ffn1_w_gate, ffn1_w_up, ffn1_w_down, ffn1_post_g, mix_pre_g, w_in, lru_conv_w, lru_conv_b, lru_w_r, lru_b_r, lru_w_i, lru_b_i, lru_lambda, sc_conv_w, w_out, mix_post_g, ffn2_pre_g, ffn2_w_gate, ffn2_w_up, ffn2_w_down, ffn2_post_g):
    bsz, seq, d = x.shape
    bf16 = lambda w: w.astype(jnp.bfloat16)
    h = x
    for l in range(ffn1_pre_g.shape[0]):
        h2d = _ffn(h.reshape(bsz * seq, d), ffn1_pre_g[l], ffn1_w_gate[l], ffn1_w_up[l],
                   ffn1_w_down[l], ffn1_post_g[l])
        w_ri = bf16(jnp.concatenate([lru_w_r[l], lru_w_i[l]], axis=-1))
        h = _mixer(h2d.reshape(bsz, seq, d), mix_pre_g[l], _pack_bf16_rows(w_in[l], PACK_COL_TILE),
                   lru_conv_w[l], lru_conv_b[l], w_ri, lru_b_r[l], lru_b_i[l], lru_lambda[l],
                   sc_conv_w[l], _pack_bf16_rows(w_out[l], PACK_COL_TILE), mix_post_g[l])
        h2d = _ffn(h.reshape(bsz * seq, d), ffn2_pre_g[l], ffn2_w_gate[l], ffn2_w_up[l],
                   ffn2_w_down[l], ffn2_post_g[l])
        h = h2d.reshape(bsz, seq, d)
    return h
```

```python
import functools
import math

import jax
import jax.numpy as jnp
from jax import lax
from jax.experimental import pallas as pl
from jax.experimental.pallas import tpu as pltpu

EPS = 1e-6
RG_C = 8.0
SUBLANES = 8
VMEM_LIMIT_BYTES = 60 * 1024 * 1024

FFN_ROW_TILE = 1024
FFN_ROW_SUBTILE = 512
FFN_EDGE_ROW_SUBTILE = 256
FFN_FF_TILE = 512
FFN_OUT_COL_TILE = 512
MIX_SEQ_TILE = 256
PACK_COL_TILE = 512


def _rms_scale(x):
    return lax.rsqrt(jnp.mean(x * x, axis=-1, keepdims=True) + EPS)


def _ffn_kernel(x_ref, pre_g_ref, wg_ref, wu_ref, wd_ref, post_g_ref, o_ref, xn_ref, hid_ref,
                *, row_sub, edge_sub, out_col_tile):
    j = pl.program_id(1)
    n_ff = pl.num_programs(1) - 1
    d_model = o_ref.shape[-1]

    row_subs = [slice(r, r + row_sub) for r in range(0, o_ref.shape[0], row_sub)]

    def hidden_rows(slot, rows):
        xn = xn_ref[rows, :]
        gate = jnp.dot(xn, wg_ref[...], preferred_element_type=jnp.float32)
        up = jnp.dot(xn, wu_ref[...], preferred_element_type=jnp.float32)
        hid_ref[slot, rows, :] = ((gate * jax.nn.sigmoid(gate)) * up).astype(hid_ref.dtype)

    def down_rows(slot, rows, first=False):
        hid = hid_ref[slot, rows, :]
        for c in range(0, d_model, out_col_tile):
            cols = slice(c, c + out_col_tile)
            part = jnp.dot(hid, wd_ref[:, cols], preferred_element_type=jnp.float32)
            if first:
                o_ref[rows, cols] = part
            else:
                o_ref[rows, cols] += part

    slot = lax.rem(j, 2)

    edge_subs = [slice(r, r + edge_sub) for r in range(0, o_ref.shape[0], edge_sub)]

    def pre_norm(rows):
        x = x_ref[rows, :]
        xn_ref[rows, :] = (x * _rms_scale(x) * pre_g_ref[...]).astype(xn_ref.dtype)

    def post_norm(rows):
        f = o_ref[rows, :]
        o_ref[rows, :] = x_ref[rows, :] + 0.5 * (f * _rms_scale(f) * post_g_ref[...])

    @pl.when(j == 0)
    def _():
        pre_norm(edge_subs[0])
        for k, rows in enumerate(edge_subs):
            if k + 1 < len(edge_subs):
                pre_norm(edge_subs[k + 1])
            hidden_rows(slot, rows)

    @pl.when(j == 1)
    def _():
        for rows in row_subs:
            down_rows(1 - slot, rows, first=True)
        for rows in row_subs:
            hidden_rows(slot, rows)

    @pl.when(jnp.logical_and(j > 1, j < n_ff))
    def _():
        for rows in row_subs:
            down_rows(1 - slot, rows)
        for rows in row_subs:
            hidden_rows(slot, rows)

    @pl.when(j == n_ff)
    def _():
        down_rows(1 - slot, edge_subs[0])
        for k, rows in enumerate(edge_subs):
            if k + 1 < len(edge_subs):
                down_rows(1 - slot, edge_subs[k + 1])
            post_norm(rows)


def _cast_chunks_kernel(w_ref, o_ref):
    o_ref[...] = w_ref[...].astype(o_ref.dtype)


def _ffn_up_weight(w, tf):
    d, f = w.shape
    return pl.pallas_call(
        _cast_chunks_kernel,
        name="cast_chunks",
        grid=(f // tf,),
        in_specs=[pl.BlockSpec((d, tf), lambda j: (0, j))],
        out_specs=pl.BlockSpec((None, d, tf), lambda j: (j, 0, 0)),
        out_shape=jax.ShapeDtypeStruct((f // tf, d, tf), jnp.bfloat16),
        compiler_params=pltpu.CompilerParams(dimension_semantics=("parallel",)),
    )(w)


def _pack_rows_kernel(w_ref, o_ref):
    o_ref[...] = pltpu.bitcast(w_ref[...].astype(jnp.bfloat16), o_ref.dtype)


def _pack_bf16_rows(w, col_tile):
    k, n = w.shape
    tn = min(col_tile, n)
    assert n % tn == 0 and k % (2 * SUBLANES) == 0
    return pl.pallas_call(
        _pack_rows_kernel,
        name="pack_rows",
        grid=(n // tn,),
        in_specs=[pl.BlockSpec((k, tn), lambda j: (0, j))],
        out_specs=pl.BlockSpec((k // 2, tn), lambda j: (0, j)),
        out_shape=jax.ShapeDtypeStruct((k // 2, n), jnp.uint32),
        compiler_params=pltpu.CompilerParams(dimension_semantics=("parallel",)),
    )(w)


def _ffn(x2d, pre_g, w_gate, w_up, w_down, post_g):
    m, d = x2d.shape
    f = w_gate.shape[1]
    tm, tf = min(FFN_ROW_TILE, m), min(FFN_FF_TILE, f)
    assert m % tm == 0 and f % tf == 0
    n_ff = f // tf
    assert n_ff >= 2
    row_sub = min(FFN_ROW_SUBTILE, tm)
    edge_sub = min(FFN_EDGE_ROW_SUBTILE, tm)
    out_col_tile = min(FFN_OUT_COL_TILE, d)
    assert tm % row_sub == 0 and tm % edge_sub == 0 and d % out_col_tile == 0
    row = lambda i, j: (i, 0)
    const = lambda i, j: (0, 0)
    up_chunk = lambda i, j: (jnp.minimum(j, n_ff - 1), 0, 0)
    down_chunk = lambda i, j: (jnp.maximum(j - 1, 0), 0)
    return pl.pallas_call(
        functools.partial(_ffn_kernel, row_sub=row_sub, edge_sub=edge_sub,
                          out_col_tile=out_col_tile),
        name="ffn",
        grid=(m // tm, n_ff + 1),
        in_specs=[
            pl.BlockSpec((tm, d), row),
            pl.BlockSpec((1, d), const),
            pl.BlockSpec((None, d, tf), up_chunk),
            pl.BlockSpec((None, d, tf), up_chunk),
            pl.BlockSpec((tf, d), down_chunk),
            pl.BlockSpec((1, d), const),
        ],
        out_specs=pl.BlockSpec((tm, d), row),
        out_shape=jax.ShapeDtypeStruct((m, d), jnp.float32),
        scratch_shapes=[
            pltpu.VMEM((tm, d), jnp.bfloat16),
            pltpu.VMEM((2, tm, tf), jnp.bfloat16),
        ],
        compiler_params=pltpu.CompilerParams(
            dimension_semantics=("parallel", "arbitrary"),
            vmem_limit_bytes=VMEM_LIMIT_BYTES),
    )(x2d, pre_g.reshape(1, d), _ffn_up_weight(w_gate, tf), _ffn_up_weight(w_up, tf),
      w_down.astype(jnp.bfloat16), post_g.reshape(1, d))


def _row_groups(x):
    t, c = x.shape
    return x.reshape(t // SUBLANES, SUBLANES, c)


def _shift_rows(cur, tail, k):
    rolled = pltpu.roll(cur, k, axis=1)
    tail_rolled = pltpu.roll(tail, k, axis=0)
    prev = jnp.concatenate([tail_rolled[None], rolled[:-1]], axis=0)
    row = lax.broadcasted_iota(jnp.int32, (1,) + tail.shape, 1)
    return jnp.where(row >= k, rolled, prev)


def _causal_conv(cur, tail, w_ref):
    k_w = w_ref.shape[0]
    y = _shift_rows(cur, tail, k_w - 1) * w_ref[0:1, :]
    for k in range(1, k_w - 1):
        y = y + _shift_rows(cur, tail, k_w - 1 - k) * w_ref[k:k + 1, :]
    return y + cur * w_ref[k_w - 1:k_w, :]


def _linear_scan(a, b, h0):
    row = lax.broadcasted_iota(jnp.int32, (1,) + a.shape[1:], 1)
    s = 1
    while s < SUBLANES:
        keep = row >= s
        b = jnp.where(keep, a * pltpu.roll(b, s, axis=1) + b, b)
        a = jnp.where(keep, a * pltpu.roll(a, s, axis=1), a)
        s *= 2
    groups = []
    carry = h0
    for g in range(a.shape[0]):
        h_g = a[g] * carry + b[g]
        groups.append(h_g)
        carry = h_g[SUBLANES - 1:SUBLANES, :]
    return jnp.concatenate(groups, axis=0)


def _gelu_tanh(x):
    c = math.sqrt(2.0 / math.pi)
    return x * (0.5 * (1.0 + jnp.tanh(c * (x + 0.044715 * (x * x * x)))))


def _mixer_kernel(h_ref, pre_g_ref, w_in_ref, lru_cw_ref, lru_cb_ref, w_ri_ref, b_r_ref, b_i_ref,
                  lam_ref, sc_cw_ref, w_out_ref, post_g_ref, o_ref,
                  lru_tail_ref, sc_tail_ref, h_carry_ref, y_ref):
    d_lru = lam_ref.shape[-1]
    d_sc = sc_cw_ref.shape[-1]
    n_heads, head_dim, _ = w_ri_ref.shape
    t = h_ref.shape[0]

    @pl.when(pl.program_id(1) == 0)
    def _():
        lru_tail_ref[...] = jnp.zeros_like(lru_tail_ref)
        sc_tail_ref[...] = jnp.zeros_like(sc_tail_ref)
        h_carry_ref[...] = jnp.zeros_like(h_carry_ref)

    x = h_ref[...]
    u = (x * _rms_scale(x) * pre_g_ref[...]).astype(jnp.bfloat16)

    def proj(col0, width):
        w = pltpu.bitcast(w_in_ref[:, col0:col0 + width], jnp.bfloat16)
        return jnp.dot(u, w, preferred_element_type=jnp.float32)

    lru_x = proj(0, d_lru)

    sc_cx = proj(2 * d_lru + d_sc, 2 * d_sc)
    v = sc_cx[:, :d_sc] * sc_cx[:, d_sc:]
    sc_conv = _causal_conv(_row_groups(v), sc_tail_ref[...], sc_cw_ref).reshape(t, d_sc)
    sc_tail_ref[...] = v[t - SUBLANES:, :]

    xc = _causal_conv(_row_groups(lru_x), lru_tail_ref[...], lru_cw_ref) + lru_cb_ref[...]
    xc = xc.reshape(t, d_lru)
    lru_tail_ref[...] = lru_x[t - SUBLANES:, :]

    neg_lam = -lam_ref[...]
    softplus_neg_lam = jnp.maximum(neg_lam, 0.0) + jnp.log1p(jnp.exp(-jnp.abs(neg_lam)))
    r_parts, i_parts = [], []
    for hd in range(n_heads):
        cols = slice(hd * head_dim, (hd + 1) * head_dim)
        ri = jnp.dot(xc[:, cols].astype(jnp.bfloat16), w_ri_ref[hd],
                     preferred_element_type=jnp.float32)
        r_parts.append(jax.nn.sigmoid(ri[:, :head_dim] + b_r_ref[:, cols]))
        i_parts.append(jax.nn.sigmoid(ri[:, head_dim:] + b_i_ref[:, cols]))
    gate_b = proj(d_lru, d_lru + d_sc)
    r = jnp.concatenate(r_parts, axis=-1)
    i = jnp.concatenate(i_parts, axis=-1)
    log_a = (-RG_C) * r * softplus_neg_lam
    a = jnp.exp(log_a)
    b = jnp.sqrt(-jnp.tanh(log_a) * (a * a + 1.0)) * (i * xc)
    h_seq = _linear_scan(_row_groups(a), _row_groups(b), h_carry_ref[SUBLANES - 1:SUBLANES, :])
    h_carry_ref[...] = h_seq[t - SUBLANES:, :]
    y_ref[:, :d_lru] = (h_seq * _gelu_tanh(gate_b[:, :d_lru])).astype(y_ref.dtype)

    y_ref[:, d_lru:] = (gate_b[:, d_lru:] * sc_conv).astype(y_ref.dtype)

    y = jnp.dot(y_ref[...], pltpu.bitcast(w_out_ref[...], jnp.bfloat16),
                preferred_element_type=jnp.float32)
    o_ref[...] = x + y * _rms_scale(y) * post_g_ref[...]


def _mixer(h, pre_g, w_in, lru_conv_w, lru_conv_b, w_ri, b_r, b_i, lam, sc_conv_w, w_out, post_g):
    bsz, seq, d = h.shape
    d_in = w_in.shape[1]
    d_lru = lam.shape[-1]
    d_sc = sc_conv_w.shape[-1]
    d_mix = d_lru + d_sc
    t = min(MIX_SEQ_TILE, seq)
    assert seq % t == 0 and t % SUBLANES == 0
    tile = lambda b, s: (b, s, 0)
    const2 = lambda b, s: (0, 0)
    const3 = lambda b, s: (0, 0, 0)
    resident = dict(pipeline_mode=pl.Buffered(1))
    return pl.pallas_call(
        _mixer_kernel,
        name="mixer",
        grid=(bsz, seq // t),
        in_specs=[
            pl.BlockSpec((None, t, d), tile),
            pl.BlockSpec((1, d), const2),
            pl.BlockSpec(w_in.shape, const2, **resident),
            pl.BlockSpec(lru_conv_w.shape, const2),
            pl.BlockSpec((1, d_lru), const2),
            pl.BlockSpec(w_ri.shape, const3, **resident),
            pl.BlockSpec((1, d_lru), const2),
            pl.BlockSpec((1, d_lru), const2),
            pl.BlockSpec((1, d_lru), const2),
            pl.BlockSpec(sc_conv_w.shape, const2),
            pl.BlockSpec(w_out.shape, const2, **resident),
            pl.BlockSpec((1, d), const2),
        ],
        out_specs=pl.BlockSpec((None, t, d), tile),
        out_shape=jax.ShapeDtypeStruct((bsz, seq, d), jnp.float32),
        scratch_shapes=[
            pltpu.VMEM((SUBLANES, d_lru), jnp.float32),
            pltpu.VMEM((SUBLANES, d_sc), jnp.float32),
            pltpu.VMEM((SUBLANES, d_lru), jnp.float32),
            pltpu.VMEM((t, d_mix), jnp.bfloat16),
        ],
        compiler_params=pltpu.CompilerParams(
            dimension_semantics=("parallel", "arbitrary"),
            vmem_limit_bytes=VMEM_LIMIT_BYTES),
    )(h, pre_g.reshape(1, d), w_in, lru_conv_w, lru_conv_b.reshape(1, d_lru), w_ri,
      b_r.reshape(1, d_lru), b_i.reshape(1, d_lru), lam.reshape(1, d_lru), sc_conv_w, w_out,
      post_g.reshape(1, d))


def kernel(x, ffn1_pre_g, ffn1_w_gate, ffn1_w_up, ffn1_w_down, ffn1_post_g, mix_pre_g, w_in, lru_conv_w, lru_conv_b, lru_w_r, lru_b_r, lru_w_i, lru_b_i, lru_lambda, sc_conv_w, w_out, mix_post_g, ffn2_pre_g, ffn2_w_gate, ffn2_w_up, ffn2_w_down, ffn2_post_g):
    bsz, seq, d = x.shape
    bf16 = lambda w: w.astype(jnp.bfloat16)
    h = x
    for l in range(ffn1_pre_g.shape[0]):
        h2d = _ffn(h.reshape(bsz * seq, d), ffn1_pre_g[l], ffn1_w_gate[l], ffn1_w_up[l],
                   ffn1_w_down[l], ffn1_post_g[l])
        w_ri = bf16(jnp.concatenate([lru_w_r[l], lru_w_i[l]], axis=-1))
        h = _mixer(h2d.reshape(bsz, seq, d), mix_pre_g[l], _pack_bf16_rows(w_in[l], PACK_COL_TILE),
                   lru_conv_w[l], lru_conv_b[l], w_ri, lru_b_r[l], lru_b_i[l], lru_lambda[l],
                   sc_conv_w[l], _pack_bf16_rows(w_out[l], PACK_COL_TILE), mix_post_g[l])
        h2d = _ffn(h.reshape(bsz * seq, d), ffn2_pre_g[l], ffn2_w_gate[l], ffn2_w_up[l],
                   ffn2_w_down[l], ffn2_post_g[l])
        h = h2d.reshape(bsz, seq, d)
    return h
```
